```python
import math
import jax, jax.numpy as jnp
from jax import lax
import numpy as np

D_MODEL = 1024
BATCH = 8
SEQ = 2048
DEPTH = 1
DEC_BATCH = 32
DEC_SEQ = 4
PAST_LEN = 16384
PAGE_SIZE = 128

D_PLE = 256
H_A = 8
DH_A = 64
HKV_A = 2
N_IDX = 4
D_IDX = 64
TOPK_MAX = 256
H_B = 4
DH_B = 64
D_FF = ((8 * D_MODEL // 3 + 255) // 256) * 256
ROPE_THETA = 10000.0
NORM_EPS = 1e-6
SUBLN_EPS = 1e-5
Q_BLOCK = 128
SPLITS = (H_A * DH_A, 2 * HKV_A * DH_A, N_IDX * D_IDX, D_IDX, N_IDX,
          2 * H_B * DH_B, 2 * H_B * DH_B, 2 * H_B * DH_B, D_MODEL, D_MODEL)
D_IN = sum(SPLITS)

kernel_name = 'hybrid_dsa_diffattn_decoder_step'


def rmsnorm(x, g, eps=NORM_EPS):
    xf = x.astype(jnp.float32)
    y = xf * lax.rsqrt(jnp.mean(xf * xf, axis=-1, keepdims=True) + eps)
    return (y * g.astype(jnp.float32)).astype(x.dtype)


def rope(x, pos):
    d = x.shape[-1]
    inv_freq = ROPE_THETA ** (-jnp.arange(0, d, 2, dtype=jnp.float32) / d)
    ang = pos.astype(jnp.float32)[:, None] * inv_freq[None, :]
    cos = jnp.cos(ang)[None, :, None, :]
    sin = jnp.sin(ang)[None, :, None, :]
    x1, x2 = jnp.split(x.astype(jnp.float32), 2, axis=-1)
    return jnp.concatenate([x1 * cos - x2 * sin, x1 * sin + x2 * cos], axis=-1).astype(x.dtype)


def project_inputs(u, w_in, pos):
    B, T, _ = u.shape
    cuts = [int(c) for c in np.cumsum(SPLITS)[:-1]]
    qa, kva, qi, ki, wi, qb, kb, vb, g_a, g_b = jnp.split(u @ w_in, cuts, axis=-1)
    qa = rope(qa.reshape(B, T, H_A, DH_A), pos)
    kva = kva.reshape(B, T, 2, HKV_A, DH_A)
    kv_a = jnp.stack([rope(kva[:, :, 0], pos), kva[:, :, 1]], axis=2)
    qi = rope(qi.reshape(B, T, N_IDX, D_IDX), pos)
    ki = rope(ki.reshape(B, T, 1, D_IDX), pos)[:, :, 0]
    qb = rope(qb.reshape(B, T, 2 * H_B, DH_B), pos).reshape(B, T, H_B, 2, DH_B)
    kb = rope(kb.reshape(B, T, 2 * H_B, DH_B), pos).reshape(B, T, H_B, 2 * DH_B)
    kv_b = jnp.stack([kb, vb.reshape(B, T, H_B, 2 * DH_B)], axis=2)
    return qa, kv_a, qi, ki, wi, qb, kv_b, g_a, g_b


def dsa_attend(qa, qi, wi, qpos, kidx_keys, gather_kv, topk):
    f32 = jnp.float32
    B, Tq = qa.shape[:2]
    L = kidx_keys.shape[1]
    dots = jnp.einsum('bqhd,bsd->bqhs', qi.astype(f32), kidx_keys.astype(f32)) * (D_IDX ** -0.5)
    score = jnp.einsum('bqh,bqhs->bqs', wi.astype(f32) * (N_IDX ** -0.5), jax.nn.relu(dots))
    admissible = jnp.arange(L, dtype=jnp.int32)[None, :] <= qpos[:, None]
    score = jnp.where(admissible[None], score, -jnp.inf)
    _, sel = lax.top_k(score, topk)
    valid = sel <= qpos[None, :, None]
    kv = gather_kv(sel).astype(f32)
    q = qa.astype(f32).reshape(B, Tq, HKV_A, H_A // HKV_A, DH_A) * (DH_A ** -0.5)
    s = jnp.einsum('bqgrd,bqkgd->bqgrk', q, kv[:, :, :, 0])
    s = jnp.where(valid[:, :, None, None, :], s, -jnp.inf)
    p = jax.nn.softmax(s, axis=-1)
    o = jnp.einsum('bqgrk,bqkgd->bqgrd', p, kv[:, :, :, 1])
    return o.reshape(B, Tq, H_A * DH_A).astype(qa.dtype)


def diff_attend(qb, segments, lam, subln_g, lam_init):
    f32 = jnp.float32
    B, Tq = qb.shape[:2]
    q = qb.astype(f32) * (DH_B ** -0.5)
    scores = []
    for k, v, mask in segments:
        s = jnp.einsum('bqhmd,bshmd->bhmqs', q, k.astype(f32))
        scores.append(jnp.where(mask[None, None, None], s, -jnp.inf))
    p = jax.nn.softmax(jnp.concatenate(scores, axis=-1), axis=-1)
    a = p[:, :, 0] - lam * p[:, :, 1]
    outs = []
    start = 0
    for k, v, mask in segments:
        L = v.shape[1]
        outs.append(jnp.einsum('bhqs,bshe->bqhe', a[..., start:start + L], v.astype(f32)))
        start += L
    o = outs[0]
    for extra in outs[1:]:
        o = o + extra
    o = rmsnorm(o, subln_g, SUBLN_EPS) * (1.0 - lam_init)
    return o.reshape(B, Tq, H_B * 2 * DH_B).astype(qb.dtype)


def diff_lambda(lq1, lk1, lq2, lk2, lam_init):
    f32 = jnp.float32
    return (jnp.exp(jnp.sum(lq1.astype(f32) * lk1.astype(f32)))
            - jnp.exp(jnp.sum(lq2.astype(f32) * lk2.astype(f32))) + lam_init)


def prompt_mixers(qa, kv_a, qi, ki, wi, qb, kv_b, lam, subln_g, lam_init):
    B, T = qa.shape[:2]
    nb = T // Q_BLOCK
    topk = min(TOPK_MAX, T // 4)
    k_b = kv_b[:, :, 0].reshape(B, T, H_B, 2, DH_B)
    v_b = kv_b[:, :, 1]
    key_idx = jnp.arange(T, dtype=jnp.int32)

    def gather_a(sel):
        return jax.vmap(lambda rows, s: rows[s])(kv_a, sel)

    def to_blocks(t):
        return jnp.moveaxis(t.reshape((B, nb, Q_BLOCK) + t.shape[2:]), 1, 0)

    def from_blocks(t):
        return jnp.moveaxis(t, 0, 1).reshape((B, T) + t.shape[3:])

    def one_block(args):
        qa_b, qi_b, wi_b, qb_b, qpos = args
        o_a = dsa_attend(qa_b, qi_b, wi_b, qpos, ki, gather_a, topk)
        mask = key_idx[None, :] <= qpos[:, None]
        o_b = diff_attend(qb_b, [(k_b, v_b, mask)], lam, subln_g, lam_init)
        return o_a, o_b

    qpos_blocks = key_idx.reshape(nb, Q_BLOCK)
    o_a, o_b = lax.map(one_block, (to_blocks(qa), to_blocks(qi), to_blocks(wi), to_blocks(qb), qpos_blocks))
    return from_blocks(o_a), from_blocks(o_b)


def sample_mixers(qa, kv_a, qi, ki, wi, qb, kv_b, pool_a_kv, pool_a_kidx, pool_b_kv, page_table,
                  lam, subln_g, lam_init):
    B, T = qa.shape[:2]
    P = page_table.shape[1] * PAGE_SIZE
    topk = min(TOPK_MAX, (P + T) // 4)
    qpos = P + jnp.arange(T, dtype=jnp.int32)
    past_kidx = pool_a_kidx[page_table].reshape(B, P, D_IDX)
    kidx_all = jnp.concatenate([past_kidx, ki.astype(past_kidx.dtype)], axis=1)
    bidx = jnp.arange(B)[:, None, None]

    def gather_a(sel):
        sp = jnp.minimum(sel, P - 1)
        rows_past = pool_a_kv[page_table[bidx, sp // PAGE_SIZE], sp % PAGE_SIZE]
        sn = jnp.clip(sel - P, 0, T - 1)
        rows_new = jax.vmap(lambda rows, s: rows[s])(kv_a, sn).astype(rows_past.dtype)
        return jnp.where((sel < P)[..., None, None, None], rows_past, rows_new)

    o_a = dsa_attend(qa, qi, wi, qpos, kidx_all, gather_a, topk)
    past_b = pool_b_kv[page_table].reshape((B, P) + pool_b_kv.shape[2:])
    k_past = past_b[:, :, 0].reshape(B, P, H_B, 2, DH_B)
    v_past = past_b[:, :, 1]
    k_new = kv_b[:, :, 0].reshape(B, T, H_B, 2, DH_B)
    v_new = kv_b[:, :, 1]
    t_idx = jnp.arange(T, dtype=jnp.int32)
    segments = [(k_past, v_past, jnp.ones((T, P), dtype=bool)),
                (k_new, v_new, t_idx[None, :] <= t_idx[:, None])]
    o_b = diff_attend(qb, segments, lam, subln_g, lam_init)
    return o_a, o_b


def layer_out(h, o_a, o_b, g_a, g_b, p_i, tail):
    (w_br_a, w_br_b, w_out, attn_post_g, ffn_pre_g, w_gate_up, w_down, ffn_post_g,
     w_ple_gate, w_ple_proj) = tail
    merged = jax.nn.sigmoid(g_a) * (o_a @ w_br_a) + jax.nn.sigmoid(g_b) * (o_b @ w_br_b)
    h = h + rmsnorm(merged @ w_out, attn_post_g)
    gate, up = jnp.split(rmsnorm(h, ffn_pre_g) @ w_gate_up, 2, axis=-1)
    h = h + rmsnorm((jax.nn.silu(gate) * up) @ w_down, ffn_post_g)
    h = h + jax.nn.sigmoid(h @ w_ple_gate) * (p_i @ w_ple_proj)
    return h


def setup_inputs(seed: int = 0) -> dict:
    key = jax.random.key(seed)
    ks = jax.random.split(key, 26)
    n_pages = PAST_LEN // PAGE_SIZE
    n_used = DEC_BATCH * n_pages
    n_pool = n_used + n_used // 4
    perm = jax.random.permutation(ks[0], n_pool)
    page_table = perm[:n_used].reshape(DEC_BATCH, n_pages).astype(jnp.int32)

    def nrm(k, shape, scale=1.0):
        return scale * jax.random.normal(k, shape, jnp.float32)

    def gain(k, n):
        return 1.0 + 0.05 * jax.random.normal(k, (DEPTH, n), jnp.float32)

    w_a = H_A * DH_A
    w_b = 2 * H_B * DH_B
    return {
        'x_prompt': nrm(ks[1], (BATCH, SEQ, D_MODEL)),
        'x_sample': nrm(ks[2], (DEC_BATCH, DEC_SEQ, D_MODEL)),
        'cache_a_kv': nrm(ks[3], (DEPTH, n_pool, PAGE_SIZE, 2, HKV_A, DH_A)),
        'cache_a_kidx': nrm(ks[4], (DEPTH, n_pool, PAGE_SIZE, D_IDX)),
        'cache_b_kv': nrm(ks[5], (DEPTH, n_pool, PAGE_SIZE, 2, H_B, 2 * DH_B)),
        'page_table': page_table,
        'p_prompt': nrm(ks[6], (DEPTH, BATCH, SEQ, D_PLE)),
        'p_sample': nrm(ks[7], (DEPTH, DEC_BATCH, DEC_SEQ, D_PLE)),
        'attn_pre_g': gain(ks[8], D_MODEL),
        'w_in': nrm(ks[9], (DEPTH, D_MODEL, D_IN), D_MODEL ** -0.5),
        'lambda_q1': nrm(ks[10], (DEPTH, DH_B), 0.1),
        'lambda_k1': nrm(ks[11], (DEPTH, DH_B), 0.1),
        'lambda_q2': nrm(ks[12], (DEPTH, DH_B), 0.1),
        'lambda_k2': nrm(ks[13], (DEPTH, DH_B), 0.1),
        'subln_g': gain(ks[14], 2 * DH_B),
        'w_br_a': nrm(ks[15], (DEPTH, w_a, D_MODEL), w_a ** -0.5),
        'w_br_b': nrm(ks[16], (DEPTH, w_b, D_MODEL), w_b ** -0.5),
        'w_out': nrm(ks[17], (DEPTH, D_MODEL, D_MODEL), D_MODEL ** -0.5),
        'attn_post_g': gain(ks[18], D_MODEL),
        'ffn_pre_g': gain(ks[19], D_MODEL),
        'w_gate_up': nrm(ks[20], (DEPTH, D_MODEL, 2 * D_FF), D_MODEL ** -0.5),
        'w_down': nrm(ks[21], (DEPTH, D_FF, D_MODEL), D_FF ** -0.5),
        'ffn_post_g': gain(ks[22], D_MODEL),
        'w_ple_gate': nrm(ks[23], (DEPTH, D_MODEL, D_MODEL), D_MODEL ** -0.5),
        'w_ple_proj': nrm(ks[24], (DEPTH, D_PLE, D_MODEL), D_PLE ** -0.5),
    }


def reference(x_prompt, x_sample, cache_a_kv, cache_a_kidx, cache_b_kv, page_table, p_prompt, p_sample,
              attn_pre_g, w_in, lambda_q1, lambda_k1, lambda_q2, lambda_k2, subln_g, w_br_a, w_br_b, w_out,
              attn_post_g, ffn_pre_g, w_gate_up, w_down, ffn_post_g, w_ple_gate, w_ple_proj):
    y_prompt, y_sample = x_prompt, x_sample
    akv_p, aki_p, bkv_p, akv_s, aki_s, bkv_s = [], [], [], [], [], []
    pos_p = jnp.arange(x_prompt.shape[1], dtype=jnp.int32)
    past_len = page_table.shape[1] * PAGE_SIZE
    pos_s = past_len + jnp.arange(x_sample.shape[1], dtype=jnp.int32)
    for i in range(DEPTH):
        lam_init = 0.8 - 0.6 * math.exp(-0.3 * i)
        lam = diff_lambda(lambda_q1[i], lambda_k1[i], lambda_q2[i], lambda_k2[i], lam_init)
        tail = (w_br_a[i], w_br_b[i], w_out[i], attn_post_g[i], ffn_pre_g[i], w_gate_up[i], w_down[i],
                ffn_post_g[i], w_ple_gate[i], w_ple_proj[i])
        qa, kv_a, qi, ki, wi, qb, kv_b, g_a, g_b = project_inputs(rmsnorm(y_prompt, attn_pre_g[i]), w_in[i], pos_p)
        o_a, o_b = prompt_mixers(qa, kv_a, qi, ki, wi, qb, kv_b, lam, subln_g[i], lam_init)
        y_prompt = layer_out(y_prompt, o_a, o_b, g_a, g_b, p_prompt[i], tail)
        akv_p.append(kv_a)
        aki_p.append(ki)
        bkv_p.append(kv_b)
        qa, kv_a, qi, ki, wi, qb, kv_b, g_a, g_b = project_inputs(rmsnorm(y_sample, attn_pre_g[i]), w_in[i], pos_s)
        o_a, o_b = sample_mixers(qa, kv_a, qi, ki, wi, qb, kv_b, cache_a_kv[i], cache_a_kidx[i], cache_b_kv[i],
                                 page_table, lam, subln_g[i], lam_init)
        y_sample = layer_out(y_sample, o_a, o_b, g_a, g_b, p_sample[i], tail)
        akv_s.append(kv_a)
        aki_s.append(ki)
        bkv_s.append(kv_b)
    return (y_prompt, y_sample, jnp.stack(akv_p), jnp.stack(aki_p), jnp.stack(bkv_p),
            jnp.stack(akv_s), jnp.stack(aki_s), jnp.stack(bkv_s))
```

```python
import functools

import jax
import jax.numpy as jnp
from jax import lax
from jax.experimental import pallas as pl
from jax.experimental.pallas import tpu as pltpu

D_MODEL = 1024
D_PLE = 256
H_A, DH_A, HKV_A = 8, 64, 2
N_IDX, D_IDX = 4, 64
TOPK_MAX = 256
H_B, DH_B = 4, 64
D_FF = 2816
ROPE_THETA = 10000.0
NORM_EPS = 1e-6
SUBLN_EPS = 1e-5
PAGE_SIZE = 128
Q_BLOCK = 128

_C_QA, _C_KVA, _C_QI, _C_KI, _C_WI, _C_QB = 0, 512, 768, 1024, 1088, 1092
_D_IN = 4676
_W1_COLS = 4352
_W2_COLS = 384
_QA_HEAD_ORDER = (0, 4, 1, 5, 2, 6, 3, 7)

_VMEM_LIMIT = 56 * 1024 * 1024
_NEG_INF = float("-inf")
_KEY_NEG_INF = -2139095041
_INT_MIN = -2147483648

_f32 = jnp.float32
_bf16 = jnp.bfloat16


def _dot_nt(a, b):
    return lax.dot_general(a, b, (((1,), (1,)), ((), ())), preferred_element_type=_f32)


def _dot(a, b):
    return jnp.dot(a, b, preferred_element_type=_f32)


def _split(x):
    hi = x.astype(_bf16)
    lo = (x - hi.astype(_f32)).astype(_bf16)
    return hi, lo


def _dot_nt3(a, b_hi, b_lo):
    a_hi, a_lo = _split(a)
    return _dot_nt(a_hi, b_hi) + (_dot_nt(a_hi, b_lo) + _dot_nt(a_lo, b_hi))


def _rms(x, g, eps):
    return x * lax.rsqrt(jnp.mean(x * x, axis=-1, keepdims=True) + eps) * g


def _sigmoid(x):
    return 1.0 / (1.0 + jnp.exp(-x))


def _float_key(x):
    b = pltpu.bitcast(x, jnp.int32)
    return b ^ ((b >> 31) & jnp.int32(0x7FFFFFFF))


def _count(mask):
    return jnp.sum(jnp.where(mask, 1.0, 0.0), axis=1, keepdims=True)


def _lambda(lamv, lam_init):
    s1 = jnp.sum(lamv[0:1, :] * lamv[1:2, :], axis=1, keepdims=True)
    s2 = jnp.sum(lamv[2:3, :] * lamv[3:4, :], axis=1, keepdims=True)
    return jnp.exp(s1) - jnp.exp(s2) + lam_init


def _rope_block(xb, cos, sin_signed, first_half):
    up = pltpu.roll(xb, 32, 1)
    dn = pltpu.roll(xb, 96, 1)
    return xb * cos + jnp.where(first_half, dn, up) * sin_signed


def _in_proj_kernel(x_ref, g_ref, cos_ref, sin_ref, w1_ref, w2h_ref, w2l_ref,
                    qa_ref, kva_ref, kvab_ref, qb_ref, kvb_ref, kvbb_ref, gab_ref, qi_ref, ki_ref, kiw_ref):
    x = x_ref[...]
    u = _rms(x, g_ref[...], NORM_EPS)
    u_hi, u_lo = _split(u)
    cos = cos_ref[...]
    sin = sin_ref[...]
    lane = lax.broadcasted_iota(jnp.int32, cos.shape, 1)
    first_half = (lane % 64) < 32

    def proj(c0, n):
        return _dot(u_hi, w1_ref[:, c0:c0 + n])

    def rope(y):
        n = y.shape[1] // 128
        return [_rope_block(y[:, 128 * j:128 * (j + 1)], cos, sin, first_half) for j in range(n)]

    y = proj(0, 512)
    for j, blk in enumerate(rope(y)):
        qa_ref[:, 128 * j:128 * (j + 1)] = (blk * 0.125).astype(_bf16)
    y = proj(512, 256)
    ka = rope(y[:, 0:128])[0]
    va = y[:, 128:256]
    kva_ref[:, 0:128] = ka
    kva_ref[:, 128:256] = va
    kvab_ref[:, 0:128] = ka.astype(_bf16)
    kvab_ref[:, 128:256] = va.astype(_bf16)
    y = proj(768, 512)
    for j, blk in enumerate(rope(y)):
        qb_ref[:, 128 * j:128 * (j + 1)] = (blk * 0.125).astype(_bf16)
    y = proj(1280, 512)
    for j, blk in enumerate(rope(y)):
        kvb_ref[:, 128 * j:128 * (j + 1)] = blk
        kvbb_ref[:, 128 * j:128 * (j + 1)] = blk.astype(_bf16)
    y = proj(1792, 512)
    kvb_ref[:, 512:1024] = y
    kvbb_ref[:, 512:1024] = y.astype(_bf16)
    gab_ref[:, 0:1024] = proj(2304, 1024).astype(_bf16)
    gab_ref[:, 1024:2048] = proj(3328, 1024).astype(_bf16)

    w2h = w2h_ref[...]
    yi = _dot(u_hi, w2h) + (_dot(u_hi, w2l_ref[...]) + _dot(u_lo, w2h))
    for j, blk in enumerate(rope(yi[:, 0:256])):
        qi_ref[:, 128 * j:128 * (j + 1)] = blk * 0.125
    raw = yi[:, 256:384]
    kiw = jnp.where(lane < 64, _rope_block(raw, cos, sin, first_half), raw * 0.5)
    kiw_ref[...] = kiw
    ki_ref[...] = kiw[:, 0:64]


def _in_proj(x2d, g, cos_t, sin_t, w1, w2h, w2l, tm):
    rows = x2d.shape[0]
    ntab = cos_t.shape[0] // tm
    row = lambda i: (i, 0)
    const = lambda i: (0, 0)
    tab = lambda i: (i % ntab, 0)
    widths = (512, 256, 256, 512, 1024, 1024, 2048, 256, 64, 128)
    dtypes = (_bf16, _f32, _bf16, _bf16, _f32, _bf16, _bf16, _f32, _f32, _f32)
    return pl.pallas_call(
        _in_proj_kernel,
        grid=(rows // tm,),
        in_specs=[pl.BlockSpec((tm, D_MODEL), row), pl.BlockSpec((1, D_MODEL), const),
                  pl.BlockSpec((tm, 128), tab), pl.BlockSpec((tm, 128), tab),
                  pl.BlockSpec((D_MODEL, _W1_COLS), const), pl.BlockSpec((D_MODEL, _W2_COLS), const),
                  pl.BlockSpec((D_MODEL, _W2_COLS), const)],
        out_specs=[pl.BlockSpec((tm, w), row) for w in widths],
        out_shape=[jax.ShapeDtypeStruct((rows, w), d) for w, d in zip(widths, dtypes)],
        compiler_params=pltpu.CompilerParams(dimension_semantics=("arbitrary",), vmem_limit_bytes=_VMEM_LIMIT),
        name="in_proj",
    )(x2d, g, cos_t, sin_t, w1, w2h, w2l)


def _kth_key(count_ge, rows, k):
    kf = jnp.float32(k)
    t0 = jnp.where(count_ge(jnp.zeros((rows, 1), jnp.int32)) >= kf, jnp.int32(0), jnp.int32(_INT_MIN))

    def body(it, t):
        cand = t | (jnp.int32(1) << (jnp.int32(30) - it))
        return jnp.where(count_ge(cand) >= kf, cand, t)

    return lax.fori_loop(0, 31, body, t0)


def _prompt_mix_kernel(lamv_ref, subg_ref, qa_ref, qb_ref, qi_ref, kiw_ref, kva_ref, ki_ref, kvb_ref,
                       oa_ref, ob_ref, key_ref, bias_ref, *, seq, topk, lam_init):
    i = pl.program_id(1)
    L = seq
    rows = Q_BLOCK
    row_pos = i * rows + lax.broadcasted_iota(jnp.int32, (rows, 1), 0)
    col = lax.broadcasted_iota(jnp.int32, (rows, L), 1)
    adm = col <= row_pos
    lane = lax.broadcasted_iota(jnp.int32, (rows, 128), 1)
    low = lane < 64

    ki_hi, ki_lo = _split(ki_ref[...])
    kiw = kiw_ref[...]
    score = jnp.zeros((rows, L), _f32)
    for h in range(N_IDX):
        d = _dot_nt3(qi_ref[:, 64 * h:64 * (h + 1)], ki_hi, ki_lo)
        score = score + kiw[:, 64 + h:65 + h] * jnp.maximum(d, 0.0)
    score = jnp.where(score == 0.0, 0.0, score)
    key_ref[...] = _float_key(jnp.where(adm, score, _NEG_INF))

    kf = jnp.float32(topk)
    t = _kth_key(lambda c: _count(key_ref[...] >= c), rows, topk)
    key = key_ref[...]
    gt = key > t
    eq = key == t
    need = kf - _count(gt)
    bias_ref[...] = jnp.where(adm & (gt | eq), 0.0, _NEG_INF)
    excess = (_count(eq) > need) & (t > _KEY_NEG_INF)

    @pl.when(jnp.max(jnp.where(excess, 1.0, 0.0)) > 0.0)
    def _():
        key = key_ref[...]
        eq = key == t
        nbits = (L - 1).bit_length()

        def body(it, pos):
            cand = pos + (jnp.int32(1) << (jnp.int32(nbits - 1) - it))
            return jnp.where(_count(eq & (col < cand)) < need, cand, pos)

        pos = lax.fori_loop(0, nbits, body, jnp.zeros((rows, 1), jnp.int32))
        bias_ref[...] = jnp.where(adm & ((key > t) | (eq & (col <= pos))), 0.0, _NEG_INF)

    def softmax_pv(s, v):
        m = jnp.max(s, axis=1, keepdims=True)
        p = jnp.exp(s - m)
        l = jnp.sum(p, axis=1, keepdims=True)
        return _dot(p.astype(_bf16), v) / l

    ka = kva_ref[:, 0:128]
    va = kva_ref[:, 128:256]
    zero = jnp.zeros((rows, 128), _bf16)
    for j in range(4):
        qblk = qa_ref[:, 128 * j:128 * (j + 1)]
        o0 = softmax_pv(_dot_nt(jnp.where(low, qblk, zero), ka) + bias_ref[...], va)
        o1 = softmax_pv(_dot_nt(jnp.where(low, zero, qblk), ka) + bias_ref[...], va)
        oa_ref[:, 128 * j:128 * (j + 1)] = jnp.where(low, o0, o1).astype(_bf16)

    lam = _lambda(lamv_ref[...], lam_init)
    cbias = jnp.where(adm, 0.0, _NEG_INF)
    for h in range(H_B):
        qblk = qb_ref[:, 128 * h:128 * (h + 1)]
        kh = kvb_ref[:, 128 * h:128 * (h + 1)]
        vh = kvb_ref[:, 512 + 128 * h:512 + 128 * (h + 1)]
        o1 = softmax_pv(_dot_nt(jnp.where(low, qblk, zero), kh) + cbias, vh)
        o2 = softmax_pv(_dot_nt(jnp.where(low, zero, qblk), kh) + cbias, vh)
        o = _rms(o1 - lam * o2, subg_ref[...], SUBLN_EPS) * (1.0 - lam_init)
        ob_ref[:, 128 * h:128 * (h + 1)] = o.astype(_bf16)


def _prompt_mix(lamv, subg, qa, qb, qi, kiw, kvab, ki, kvbb, batch, seq, lam_init):
    nq = seq // Q_BLOCK
    topk = min(TOPK_MAX, seq // 4)
    qrow = lambda b, i: (b * nq + i, 0)
    brow = lambda b, i: (b, 0)
    const = lambda b, i: (0, 0)
    kern = functools.partial(_prompt_mix_kernel, seq=seq, topk=topk, lam_init=lam_init)
    return pl.pallas_call(
        kern,
        grid=(batch, nq),
        in_specs=[pl.BlockSpec((4, DH_B), const), pl.BlockSpec((1, 2 * DH_B), const),
                  pl.BlockSpec((Q_BLOCK, 512), qrow), pl.BlockSpec((Q_BLOCK, 512), qrow),
                  pl.BlockSpec((Q_BLOCK, 256), qrow), pl.BlockSpec((Q_BLOCK, 128), qrow),
                  pl.BlockSpec((seq, 256), brow), pl.BlockSpec((seq, 64), brow), pl.BlockSpec((seq, 1024), brow)],
        out_specs=[pl.BlockSpec((Q_BLOCK, 512), qrow), pl.BlockSpec((Q_BLOCK, 512), qrow)],
        out_shape=[jax.ShapeDtypeStruct((batch * seq, 512), _bf16)] * 2,
        scratch_shapes=[pltpu.VMEM((Q_BLOCK, seq), jnp.int32), pltpu.VMEM((Q_BLOCK, seq), _f32)],
        compiler_params=pltpu.CompilerParams(dimension_semantics=("arbitrary", "arbitrary"),
                                             vmem_limit_bytes=_VMEM_LIMIT),
        name="prompt_mix",
    )(lamv, subg, qa, qb, qi, kiw, kvab, ki, kvbb)


def _page_copies(pt_ref, pool_ref, buf_ref, sem, b, first_page, n_pages, slot):
    def copy(pg):
        page = pt_ref[b, first_page + pg]
        dst = buf_ref.at[slot, pl.ds(pl.multiple_of(pg * PAGE_SIZE, PAGE_SIZE), PAGE_SIZE), :]
        return pltpu.make_async_copy(pool_ref.at[page], dst, sem.at[slot])
    return copy


def _start_pages(copy, n_pages):
    def body(pg, c):
        copy(pg).start()
        return c
    lax.fori_loop(0, n_pages, body, 0)


def _wait_pages(copy, n_pages):
    def body(pg, c):
        copy(pg).wait()
        return c
    lax.fori_loop(0, n_pages, body, 0)


def _sample_idx_kernel(pt_ref, q_ref, w_ref, knew_ref, pool_ref, bpast_ref, bnew_ref, kbuf, sem, key_ref,
                       *, n_pages, topk, dec_seq):
    b = pl.program_id(0)
    nb = pl.num_programs(0)
    slot = b % 2
    P = n_pages * PAGE_SIZE

    @pl.when(b == 0)
    def _():
        _start_pages(_page_copies(pt_ref, pool_ref, kbuf, sem, b, 0, n_pages, slot), n_pages)

    @pl.when(b + 1 < nb)
    def _():
        _start_pages(_page_copies(pt_ref, pool_ref, kbuf, sem, b + 1, 0, n_pages, 1 - slot), n_pages)

    _wait_pages(_page_copies(pt_ref, pool_ref, kbuf, sem, b, 0, n_pages, slot), n_pages)

    q = q_ref[0]
    w = w_ref[0]
    k_hi, k_lo = _split(kbuf[slot])
    dp = _dot_nt3(q, k_hi, k_lo)
    n_hi, n_lo = _split(knew_ref[0])
    dn = _dot_nt3(q, n_hi, n_lo)
    sp = jnp.zeros((8, P), _f32)
    sn = jnp.zeros((8, 128), _f32)
    for h in range(N_IDX):
        wh = w[:, 64 + h:65 + h]
        sp = sp + wh * jnp.maximum(dp[8 * h:8 * (h + 1), :], 0.0)
        sn = sn + wh * jnp.maximum(dn[8 * h:8 * (h + 1), :], 0.0)
    sp = jnp.where(sp == 0.0, 0.0, sp)
    sn = jnp.where(sn == 0.0, 0.0, sn)
    tq = lax.broadcasted_iota(jnp.int32, (8, 1), 0) % dec_seq
    coln = lax.broadcasted_iota(jnp.int32, (8, 128), 1)
    admn = coln <= tq
    key_ref[...] = _float_key(sp)
    keyn = _float_key(jnp.where(admn, sn, _NEG_INF))

    kf = jnp.float32(topk)
    t = _kth_key(lambda c: _count(key_ref[...] >= c) + _count(keyn >= c), 8, topk)
    keyp = key_ref[...]
    need = kf - (_count(keyp > t) + _count(keyn > t))
    eqp = keyp == t
    eqn = (keyn == t) & admn
    colp = lax.broadcasted_iota(jnp.int32, (8, P), 1)
    nbits = (P + 127).bit_length()

    def body(it, pos):
        cand = pos + (jnp.int32(1) << (jnp.int32(nbits - 1) - it))
        c = _count(eqp & (colp < cand)) + _count(eqn & (coln + P < cand))
        return jnp.where(c < need, cand, pos)

    pos = lax.fori_loop(0, nbits, body, jnp.zeros((8, 1), jnp.int32))
    bpast_ref[0] = jnp.where((keyp > t) | (eqp & (colp <= pos)), 0.0, _NEG_INF)
    bnew_ref[0] = jnp.where(admn & ((keyn > t) | (eqn & (coln + P <= pos))), 0.0, _NEG_INF)


def _sample_idx(page_table, q32, w8, knew, pool_kidx, topk, dec_seq):
    nb, n_pages = page_table.shape
    P = n_pages * PAGE_SIZE
    blk = lambda b, pt: (b, 0, 0)
    kern = functools.partial(_sample_idx_kernel, n_pages=n_pages, topk=topk, dec_seq=dec_seq)
    return pl.pallas_call(
        kern,
        grid_spec=pltpu.PrefetchScalarGridSpec(
            num_scalar_prefetch=1,
            grid=(nb,),
            in_specs=[pl.BlockSpec((1, 32, 64), blk), pl.BlockSpec((1, 8, 128), blk),
                      pl.BlockSpec((1, 128, 64), blk), pl.BlockSpec(memory_space=pl.ANY)],
            out_specs=[pl.BlockSpec((1, 8, P), blk), pl.BlockSpec((1, 8, 128), blk)],
            scratch_shapes=[pltpu.VMEM((2, P, D_IDX), _f32), pltpu.SemaphoreType.DMA((2,)),
                            pltpu.VMEM((8, P), jnp.int32)]),
        out_shape=[jax.ShapeDtypeStruct((nb, 8, P), _f32), jax.ShapeDtypeStruct((nb, 8, 128), _f32)],
        compiler_params=pltpu.CompilerParams(dimension_semantics=("arbitrary",), vmem_limit_bytes=_VMEM_LIMIT),
        name="sample_idx",
    )(page_table, q32, w8, knew, pool_kidx)


_CHUNK_PAGES = 16


def _sample_attn_kernel(pt_ref, lamv_ref, subg_ref, qa_ref, qb_ref, bpast_ref, bnew_ref, kvan_ref, kvbn_ref,
                        poola_ref, poolb_ref, oa_ref, ob_ref,
                        abuf, bbuf, sema, semb, ma, la, acca, mb, lb, accb, *, n_chunks, dec_seq, lam_init):
    b = pl.program_id(0)
    c = pl.program_id(1)
    nb = pl.num_programs(0)
    step = b * n_chunks + c
    slot = step % 2
    npg = _CHUNK_PAGES

    def start(bb, cc, sl):
        _start_pages(_page_copies(pt_ref, poola_ref, abuf, sema, bb, cc * npg, npg, sl), npg)
        _start_pages(_page_copies(pt_ref, poolb_ref, bbuf, semb, bb, cc * npg, npg, sl), npg)

    @pl.when(step == 0)
    def _():
        start(b, c, slot)

    @pl.when(step + 1 < nb * n_chunks)
    def _():
        nxt = step + 1
        start(nxt // n_chunks, nxt % n_chunks, 1 - slot)

    @pl.when(c == 0)
    def _():
        ma[...] = jnp.full(ma.shape, _NEG_INF, _f32)
        mb[...] = jnp.full(mb.shape, _NEG_INF, _f32)
        la[...] = jnp.zeros(la.shape, _f32)
        lb[...] = jnp.zeros(lb.shape, _f32)
        acca[...] = jnp.zeros(acca.shape, _f32)
        accb[...] = jnp.zeros(accb.shape, _f32)

    _wait_pages(_page_copies(pt_ref, poola_ref, abuf, sema, b, c * npg, npg, slot), npg)
    _wait_pages(_page_copies(pt_ref, poolb_ref, bbuf, semb, b, c * npg, npg, slot), npg)

    qa = qa_ref[0]
    qb = qb_ref[0]

    def update(m_ref, l_ref, acc_ref, s, v):
        m_old = m_ref[...]
        m_new = jnp.maximum(m_old, jnp.max(s, axis=1, keepdims=True))
        m_safe = jnp.where(m_new == _NEG_INF, 0.0, m_new)
        alpha = jnp.exp(m_old - m_safe)
        p = jnp.exp(s - m_safe)
        l_ref[...] = alpha * l_ref[...] + jnp.sum(p, axis=1, keepdims=True)
        acc_ref[...] = alpha * acc_ref[...] + _dot(p.astype(_bf16), v)
        m_ref[...] = m_new

    def tile4(x):
        return jnp.concatenate([x, x, x, x], axis=0)

    ka = abuf[slot, :, 0:128].astype(_bf16)
    va = abuf[slot, :, 128:256].astype(_bf16)
    update(ma, la, acca, _dot_nt(qa, ka) + tile4(bpast_ref[0]), va)
    kb = bbuf[slot, :, 0:512].astype(_bf16)
    vb = bbuf[slot, :, 512:1024].astype(_bf16)
    update(mb, lb, accb, _dot_nt(qb, kb), vb)

    @pl.when(c == n_chunks - 1)
    def _():
        kan = kvan_ref[0, :, 0:128].astype(_bf16)
        van = kvan_ref[0, :, 128:256].astype(_bf16)
        update(ma, la, acca, _dot_nt(qa, kan) + tile4(bnew_ref[0]), van)
        kbn = kvbn_ref[0, :, 0:512].astype(_bf16)
        vbn = kvbn_ref[0, :, 512:1024].astype(_bf16)
        tq = lax.broadcasted_iota(jnp.int32, (32, 128), 0) % dec_seq
        coln = lax.broadcasted_iota(jnp.int32, (32, 128), 1)
        update(mb, lb, accb, _dot_nt(qb, kbn) + jnp.where(coln <= tq, 0.0, _NEG_INF), vbn)

        oa_ref[0] = acca[...] / la[...]
        lam = _lambda(lamv_ref[...], lam_init)
        ob = accb[...] / lb[...]
        for h in range(H_B):
            blk = ob[8 * h:8 * (h + 1), 128 * h:128 * (h + 1)]
            o = blk - lam * pltpu.roll(blk, 4, 0)
            ob_ref[0, 8 * h:8 * (h + 1), :] = _rms(o, subg_ref[...], SUBLN_EPS) * (1.0 - lam_init)


def _sample_attn(page_table, lamv, subg, qa_bd, qb_bd, bias_past, bias_new, kva_new, kvb_new, pool_a, pool_b,
                 dec_seq, lam_init):
    nb, n_pages = page_table.shape
    n_chunks = n_pages // _CHUNK_PAGES
    ch = _CHUNK_PAGES * PAGE_SIZE
    blk = lambda b, c, pt: (b, 0, 0)
    const = lambda b, c, pt: (0, 0)
    kern = functools.partial(_sample_attn_kernel, n_chunks=n_chunks, dec_seq=dec_seq, lam_init=lam_init)
    return pl.pallas_call(
        kern,
        grid_spec=pltpu.PrefetchScalarGridSpec(
            num_scalar_prefetch=1,
            grid=(nb, n_chunks),
            in_specs=[pl.BlockSpec((4, DH_B), const), pl.BlockSpec((1, 2 * DH_B), const),
                      pl.BlockSpec((1, 32, 128), blk), pl.BlockSpec((1, 32, 512), blk),
                      pl.BlockSpec((1, 8, ch), lambda b, c, pt: (b, 0, c)), pl.BlockSpec((1, 8, 128), blk),
                      pl.BlockSpec((1, 128, 256), blk), pl.BlockSpec((1, 128, 1024), blk),
                      pl.BlockSpec(memory_space=pl.ANY), pl.BlockSpec(memory_space=pl.ANY)],
            out_specs=[pl.BlockSpec((1, 32, 128), blk), pl.BlockSpec((1, 32, 128), blk)],
            scratch_shapes=[pltpu.VMEM((2, ch, 256), _f32), pltpu.VMEM((2, ch, 1024), _f32),
                            pltpu.SemaphoreType.DMA((2,)), pltpu.SemaphoreType.DMA((2,)),
                            pltpu.VMEM((32, 1), _f32), pltpu.VMEM((32, 1), _f32), pltpu.VMEM((32, 128), _f32),
                            pltpu.VMEM((32, 1), _f32), pltpu.VMEM((32, 1), _f32), pltpu.VMEM((32, 512), _f32)]),
        out_shape=[jax.ShapeDtypeStruct((nb, 32, 128), _f32), jax.ShapeDtypeStruct((nb, 32, 128), _f32)],
        compiler_params=pltpu.CompilerParams(dimension_semantics=("arbitrary", "arbitrary"),
                                             vmem_limit_bytes=_VMEM_LIMIT),
        name="sample_attn",
    )(page_table, lamv, subg, qa_bd, qb_bd, bias_past, bias_new, kva_new, kvb_new, pool_a, pool_b)


_FF_CHUNK = 256


def _layer_out_kernel(x_ref, oa_ref, ob_ref, gab_ref, p_ref, gains_ref, wbra_ref, wbrb_ref, wout_ref,
                      wgu_ref, wd_ref, wpg_ref, wpp_ref, y_ref):
    a = _dot(oa_ref[...], wbra_ref[...])
    bb = _dot(ob_ref[...], wbrb_ref[...])
    ga = gab_ref[:, 0:1024].astype(_f32)
    gb = gab_ref[:, 1024:2048].astype(_f32)
    merged = _sigmoid(ga) * a + _sigmoid(gb) * bb
    h = x_ref[...] + _rms(_dot(merged.astype(_bf16), wout_ref[...]), gains_ref[0:1, :], NORM_EPS)
    n = _rms(h, gains_ref[1:2, :], NORM_EPS).astype(_bf16)
    acc = jnp.zeros(h.shape, _f32)
    for c in range(D_FF // _FF_CHUNK):
        g = _dot(n, wgu_ref[:, _FF_CHUNK * c:_FF_CHUNK * (c + 1)])
        u = _dot(n, wgu_ref[:, D_FF + _FF_CHUNK * c:D_FF + _FF_CHUNK * (c + 1)])
        acc = acc + _dot((g * _sigmoid(g) * u).astype(_bf16), wd_ref[_FF_CHUNK * c:_FF_CHUNK * (c + 1), :])
    h = h + _rms(acc, gains_ref[2:3, :], NORM_EPS)
    gate = _sigmoid(_dot(h.astype(_bf16), wpg_ref[...]))
    y_ref[...] = h + gate * _dot(p_ref[...].astype(_bf16), wpp_ref[...])


def _layer_out(x2d, oa, ob, gab, p2d, gains, wbra, wbrb, wout, wgu, wd, wpg, wpp, tm):
    rows = x2d.shape[0]
    row = lambda i: (i, 0)
    const = lambda i: (0, 0)

    def wspec(w):
        return pl.BlockSpec(w.shape, const, pipeline_mode=pl.Buffered(1))

    return pl.pallas_call(
        _layer_out_kernel,
        grid=(rows // tm,),
        in_specs=[pl.BlockSpec((tm, D_MODEL), row), pl.BlockSpec((tm, 512), row), pl.BlockSpec((tm, 512), row),
                  pl.BlockSpec((tm, 2048), row), pl.BlockSpec((tm, D_PLE), row), pl.BlockSpec((3, D_MODEL), const)]
                 + [wspec(w) for w in (wbra, wbrb, wout, wgu, wd, wpg, wpp)],
        out_specs=pl.BlockSpec((tm, D_MODEL), row),
        out_shape=jax.ShapeDtypeStruct((rows, D_MODEL), _f32),
        compiler_params=pltpu.CompilerParams(dimension_semantics=("arbitrary",), vmem_limit_bytes=_VMEM_LIMIT),
        name="layer_out",
    )(x2d, oa, ob, gab, p2d, gains, wbra, wbrb, wout, wgu, wd, wpg, wpp)


def _rope_tables(pos):
    inv_freq = ROPE_THETA ** (-jnp.arange(0, 64, 2, dtype=_f32) / 64)
    ang = pos.astype(_f32)[:, None] * inv_freq[None, :]
    cos, sin = jnp.cos(ang), jnp.sin(ang)
    return jnp.concatenate([cos, cos, cos, cos], axis=1), jnp.concatenate([-sin, sin, -sin, sin], axis=1)


def _head_perm_cols(order, width):
    return jnp.concatenate([jnp.arange(h * width, (h + 1) * width) for h in order])


def kernel(x_prompt, x_sample, cache_a_kv, cache_a_kidx, cache_b_kv, page_table, p_prompt, p_sample, attn_pre_g, w_in, lambda_q1, lambda_k1, lambda_q2, lambda_k2, subln_g, w_br_a, w_br_b, w_out, attn_post_g, ffn_pre_g, w_gate_up, w_down, ffn_post_g, w_ple_gate, w_ple_proj):
    batch, seq, _ = x_prompt.shape
    nb, dec_seq, _ = x_sample.shape
    n_pages = page_table.shape[1]
    past = n_pages * PAGE_SIZE
    lam_init = 0.2

    w = w_in[0]
    qa_cols = _head_perm_cols(_QA_HEAD_ORDER, DH_A)
    w1 = jnp.concatenate([w[:, qa_cols], w[:, _C_KVA:_C_QI], w[:, _C_QB:_D_IN]], axis=1).astype(_bf16)
    w2 = jnp.pad(w[:, _C_QI:_C_QB], ((0, 0), (0, _W2_COLS - (_C_QB - _C_QI))))
    w2h = w2.astype(_bf16)
    w2l = (w2 - w2h.astype(_f32)).astype(_bf16)
    wbra = w_br_a[0][qa_cols, :].astype(_bf16)
    wbrb = w_br_b[0].astype(_bf16)
    wout = w_out[0].astype(_bf16)
    wgu = w_gate_up[0].astype(_bf16)
    wd = w_down[0].astype(_bf16)
    wpg = w_ple_gate[0].astype(_bf16)
    wpp = w_ple_proj[0].astype(_bf16)
    gains = jnp.concatenate([attn_post_g, ffn_pre_g, ffn_post_g], axis=0)
    lamv = jnp.concatenate([lambda_q1, lambda_k1, lambda_q2, lambda_k2], axis=0)
    cos_p, sin_p = _rope_tables(jnp.arange(seq, dtype=jnp.int32))
    cos_s, sin_s = _rope_tables(past + jnp.arange(nb * dec_seq, dtype=jnp.int32) % dec_seq)

    xp = x_prompt.reshape(batch * seq, D_MODEL)
    tm = 256 if seq % 256 == 0 else 128
    qa, kva, kvab, qb, kvb, kvbb, gab, qi, ki, kiw = _in_proj(xp, attn_pre_g, cos_p, sin_p, w1, w2h, w2l, tm)
    oa, ob = _prompt_mix(lamv, subln_g, qa, qb, qi, kiw, kvab, ki, kvbb, batch, seq, lam_init)
    y_prompt = _layer_out(xp, oa, ob, gab, p_prompt[0].reshape(batch * seq, D_PLE), gains,
                          wbra, wbrb, wout, wgu, wd, wpg, wpp, tm).reshape(batch, seq, D_MODEL)

    rs = nb * dec_seq
    xs = x_sample.reshape(rs, D_MODEL)
    qa_s, kva_s, _, qb_s, kvb_s, _, gab_s, qi_s, ki_s, kiw_s = _in_proj(xs, attn_pre_g, cos_s, sin_s, w1, w2h, w2l, rs)
    topk_s = min(TOPK_MAX, (past + dec_seq) // 4)
    reps = 8 // dec_seq
    q32 = jnp.tile(qi_s.reshape(nb, dec_seq, N_IDX, D_IDX).transpose(0, 2, 1, 3), (1, 1, reps, 1)).reshape(nb, 32, D_IDX)
    w8 = jnp.tile(kiw_s.reshape(nb, dec_seq, 128), (1, reps, 1))
    knew = jnp.pad(ki_s.reshape(nb, dec_seq, D_IDX), ((0, 0), (0, 128 - dec_seq), (0, 0)))
    bias_past, bias_new = _sample_idx(page_table, q32, w8, knew, cache_a_kidx[0], topk_s, dec_seq)

    qa_h = qa_s.reshape(nb, dec_seq, 4, 2, DH_A)
    zero_a = jnp.zeros_like(qa_h[:, :, :, 0])
    qa_g0 = jnp.concatenate([qa_h[:, :, :, 0], zero_a], axis=-1)
    qa_g1 = jnp.concatenate([zero_a, qa_h[:, :, :, 1]], axis=-1)
    qa_bd = jnp.stack([qa_g0, qa_g1], axis=1).transpose(0, 1, 3, 2, 4).reshape(nb, 32, 128)
    qb_h = qb_s.reshape(nb, dec_seq, H_B, 2, DH_B).transpose(0, 2, 3, 1, 4)
    eye = jnp.eye(2 * H_B, dtype=_bf16).reshape(H_B, 2, 1, 2 * H_B, 1)
    qb_bd = (qb_h[:, :, :, :, None, :] * eye[None]).reshape(nb, 32, 2 * H_B * DH_B)
    kva_new = jnp.pad(kva_s.reshape(nb, dec_seq, 256), ((0, 0), (0, 128 - dec_seq), (0, 0)))
    kvb_new = jnp.pad(kvb_s.reshape(nb, dec_seq, 1024), ((0, 0), (0, 128 - dec_seq), (0, 0)))
    pool_a = cache_a_kv[0].reshape(-1, PAGE_SIZE, 2 * HKV_A * DH_A)
    pool_b = cache_b_kv[0].reshape(-1, PAGE_SIZE, 4 * H_B * DH_B)
    oa_raw, ob_raw = _sample_attn(page_table, lamv, subln_g, qa_bd, qb_bd, bias_past, bias_new, kva_new, kvb_new,
                                  pool_a, pool_b, dec_seq, lam_init)
    oa5 = oa_raw.reshape(nb, 2, 4, dec_seq, 2, DH_A)
    oa_sel = jnp.stack([oa5[:, 0, :, :, 0], oa5[:, 1, :, :, 1]], axis=1)
    oa_s = oa_sel.transpose(0, 3, 2, 1, 4).reshape(rs, 512).astype(_bf16)
    ob_s = ob_raw.reshape(nb, H_B, 8, 128)[:, :, :dec_seq].transpose(0, 2, 1, 3).reshape(rs, 512).astype(_bf16)
    y_sample = _layer_out(xs, oa_s, ob_s, gab_s, p_sample[0].reshape(rs, D_PLE), gains,
                          wbra, wbrb, wout, wgu, wd, wpg, wpp, rs).reshape(nb, dec_seq, D_MODEL)

    return (y_prompt, y_sample,
            kva.reshape(1, batch, seq, 2, HKV_A, DH_A), ki.reshape(1, batch, seq, D_IDX),
            kvb.reshape(1, batch, seq, 2, H_B, 2 * DH_B),
            kva_s.reshape(1, nb, dec_seq, 2, HKV_A, DH_A), ki_s.reshape(1, nb, dec_seq, D_IDX),
            kvb_s.reshape(1, nb, dec_seq, 2, H_B, 2 * DH_B))
```
